```python
import jax, jax.numpy as jnp
from jax import lax
import numpy as np

D_MODEL = 2048
BATCH = 4
SEQ = 4096
DEPTH = 2

N_MEM = 256
EPS = 1e-6
CONV_WIDTH = D_MODEL // 2
CONV_K = 3
GLA_HEADS = 4
GLA_DK = D_MODEL // 2
GLA_DV = D_MODEL
GLA_RANK = 16
GLA_TAU = 16.0
GLA_CHUNK = 64
MEM_HEADS = 4
MEM_WIDTH = D_MODEL // 2
N_BRANCH = 3
N_GROUPS = 4
EXPERTS_PER_GROUP = 4
N_EXPERTS = N_GROUPS * EXPERTS_PER_GROUP
TOP_K = 2
D_EXPERT = D_MODEL // 4
IN_SPLITS = (CONV_WIDTH, CONV_WIDTH, CONV_WIDTH, GLA_DK, GLA_DK, GLA_DV, GLA_DV, GLA_RANK, MEM_WIDTH, D_MODEL, D_MODEL, D_MODEL)
D_IN = 3 * CONV_WIDTH + 2 * GLA_DK + 2 * GLA_DV + GLA_RANK + MEM_WIDTH + N_BRANCH * D_MODEL

kernel_name = "hybrid_gated_conv_gla_memxattn_hiermoe"


def rms_norm(x, g):
    xf = x.astype(jnp.float32)
    y = xf * lax.rsqrt(jnp.mean(xf * xf, axis=-1, keepdims=True) + EPS)
    return (y * g.astype(jnp.float32)).astype(x.dtype)


def split_columns(z):
    idx = np.cumsum(np.array(IN_SPLITS))[:-1].tolist()
    return jnp.split(z, idx, axis=-1)


def short_conv_mixer(gb, gc, v, w_conv):
    u = gc * v
    s = u.shape[1]
    y = w_conv[0] * u
    for k in range(1, CONV_K):
        y = y + w_conv[k] * jnp.pad(u, ((0, 0), (k, 0), (0, 0)))[:, :s]
    return gb * y


def gla_mixer(q, k, v, r, a_lr, w_a2, b_a2, g_norm):
    f32 = jnp.float32
    bsz, s, _ = q.shape
    H, C = GLA_HEADS, GLA_CHUNK
    dk, dv = GLA_DK // H, GLA_DV // H
    n = s // C
    log_a = jax.nn.log_sigmoid((a_lr @ w_a2 + b_a2).astype(f32)) / GLA_TAU

    def heads(t, d):
        return t.astype(f32).reshape(bsz, n, C, H, d).transpose(0, 3, 1, 2, 4)

    qh = heads(q, dk) * (dk ** -0.5)
    kh = heads(k, dk)
    vh = heads(v, dv)
    bcum = jnp.cumsum(heads(log_a, dk), axis=3)
    b_last = bcum[:, :, :, -1:]
    q_dec = qh * jnp.exp(bcum)
    k_dec = kh * jnp.exp(-bcum)
    k_to_end = kh * jnp.exp(b_last - bcum)
    mask = jnp.tril(jnp.ones((C, C), dtype=bool))
    att = jnp.where(mask, jnp.einsum('bhnid,bhnjd->bhnij', q_dec, k_dec), 0.0)
    o_intra = jnp.einsum('bhnij,bhnjv->bhniv', att, vh)

    def step(state, xs):
        qd, kd, vv, bl = xs
        o = jnp.einsum('bhcd,bhdv->bhcv', qd, state)
        state = jnp.exp(bl[:, :, 0, :])[..., None] * state + jnp.einsum('bhcd,bhcv->bhdv', kd, vv)
        return state, o

    xs = (jnp.moveaxis(q_dec, 2, 0), jnp.moveaxis(k_to_end, 2, 0), jnp.moveaxis(vh, 2, 0), jnp.moveaxis(b_last, 2, 0))
    s0 = jnp.zeros((bsz, H, dk, dv), f32)
    _, o_inter = lax.scan(step, s0, xs)
    o = o_intra + jnp.moveaxis(o_inter, 0, 2)
    o = o * lax.rsqrt(jnp.mean(o * o, axis=-1, keepdims=True) + EPS)
    o = o.transpose(0, 2, 3, 1, 4).reshape(bsz, s, GLA_DV) * g_norm.astype(f32)
    return (o * jax.nn.silu(r.astype(f32))).astype(q.dtype)


def memory_cross_attention(q, mem_n, w_mem_kv):
    bsz, s, _ = q.shape
    hd = MEM_WIDTH // MEM_HEADS
    kmem, vmem = jnp.split(mem_n @ w_mem_kv, 2, axis=-1)
    qh = q.reshape(bsz, s, MEM_HEADS, hd)
    kh = kmem.reshape(bsz, N_MEM, MEM_HEADS, hd)
    vh = vmem.reshape(bsz, N_MEM, MEM_HEADS, hd)
    sc = jnp.einsum('bshd,bmhd->bhsm', qh, kh).astype(jnp.float32) * (hd ** -0.5)
    p = jax.nn.softmax(sc, axis=-1).astype(q.dtype)
    return jnp.einsum('bhsm,bmhd->bshd', p, vh).reshape(bsz, s, MEM_WIDTH)


def hybrid_mixer(h, mem, w_in, conv_w, gla_w_a2, gla_b_a2, gla_norm, norm_mem, w_mem_kv,
                 w_br_conv, w_br_gla, w_br_mem, w_o):
    f32 = jnp.float32
    z = h @ w_in
    cb, cc, cv, gq, gk, gv, gr, ga, mq, gate_a, gate_b, gate_m = split_columns(z)
    y_a = short_conv_mixer(cb, cc, cv, conv_w)
    y_b = gla_mixer(gq, gk, gv, gr, ga, gla_w_a2, gla_b_a2, gla_norm)
    y_m = memory_cross_attention(mq, rms_norm(mem, norm_mem), w_mem_kv)
    merged = (jax.nn.sigmoid(gate_a.astype(f32)) * (y_a @ w_br_conv).astype(f32)
              + jax.nn.sigmoid(gate_b.astype(f32)) * (y_b @ w_br_gla).astype(f32)
              + jax.nn.sigmoid(gate_m.astype(f32)) * (y_m @ w_br_mem).astype(f32))
    return merged.astype(h.dtype) @ w_o


def hierarchical_moe(h, w_rg, w_re, w_gate, w_up, w_down):
    f32 = jnp.float32
    bsz, s, d = h.shape
    t = h.reshape(bsz * s, d)
    pg = jax.nn.softmax((t @ w_rg).astype(f32), axis=-1)
    gsel = jnp.argmax(pg, axis=-1)
    wg = jnp.take_along_axis(pg, gsel[:, None], axis=-1)
    le = (t @ w_re).astype(f32).reshape(-1, N_GROUPS, EXPERTS_PER_GROUP)
    le_sel = jnp.take_along_axis(le, gsel[:, None, None], axis=1)[:, 0]
    top_p, top_i = lax.top_k(jax.nn.softmax(le_sel, axis=-1), TOP_K)
    top_p = top_p / jnp.sum(top_p, axis=-1, keepdims=True)
    eid = gsel[:, None] * EXPERTS_PER_GROUP + top_i
    comb = jnp.sum(jax.nn.one_hot(eid, N_EXPERTS, dtype=f32) * (wg * top_p)[..., None], axis=1)
    out = jnp.zeros((bsz * s, d), f32)
    for e in range(N_EXPERTS):
        hid = jax.nn.silu(t @ w_gate[e]) * (t @ w_up[e])
        out = out + comb[:, e:e + 1] * (hid @ w_down[e]).astype(f32)
    return out.astype(h.dtype).reshape(bsz, s, d)


def _normal(key, shape, scale):
    return jax.random.normal(key, shape, jnp.float32) * scale


def setup_inputs(seed: int = 0) -> dict:
    key = jax.random.key(seed)
    ks = jax.random.split(key, 24)
    D = D_MODEL
    gain = lambda k, shape: 1.0 + _normal(k, shape, 0.02)
    return {
        "x": _normal(ks[0], (BATCH, SEQ, D), 1.0),
        "mem": _normal(ks[1], (BATCH, N_MEM, D), 1.0),
        "norm_mix": gain(ks[2], (DEPTH, D)),
        "w_in": _normal(ks[3], (DEPTH, D, D_IN), D ** -0.5),
        "conv_w": _normal(ks[4], (DEPTH, CONV_K, CONV_WIDTH), CONV_K ** -0.5),
        "gla_w_a2": _normal(ks[5], (DEPTH, GLA_RANK, GLA_DK), GLA_RANK ** -0.5),
        "gla_b_a2": _normal(ks[6], (DEPTH, GLA_DK), 0.1),
        "gla_norm": gain(ks[7], (DEPTH, GLA_DV)),
        "norm_mem": gain(ks[8], (DEPTH, D)),
        "w_mem_kv": _normal(ks[9], (DEPTH, D, 2 * MEM_WIDTH), D ** -0.5),
        "w_br_conv": _normal(ks[10], (DEPTH, CONV_WIDTH, D), CONV_WIDTH ** -0.5),
        "w_br_gla": _normal(ks[11], (DEPTH, GLA_DV, D), GLA_DV ** -0.5),
        "w_br_mem": _normal(ks[12], (DEPTH, MEM_WIDTH, D), MEM_WIDTH ** -0.5),
        "w_o": _normal(ks[13], (DEPTH, D, D), D ** -0.5),
        "norm_ffn": gain(ks[14], (DEPTH, D)),
        "w_router_group": _normal(ks[15], (DEPTH, D, N_GROUPS), D ** -0.5),
        "w_router_expert": _normal(ks[16], (DEPTH, D, N_EXPERTS), D ** -0.5),
        "w_gate": _normal(ks[17], (DEPTH, N_EXPERTS, D, D_EXPERT), D ** -0.5),
        "w_up": _normal(ks[18], (DEPTH, N_EXPERTS, D, D_EXPERT), D ** -0.5),
        "w_down": _normal(ks[19], (DEPTH, N_EXPERTS, D_EXPERT, D), D_EXPERT ** -0.5),
        "norm_final": gain(ks[20], (D,)),
    }


def reference(x, mem, norm_mix, w_in, conv_w, gla_w_a2, gla_b_a2, gla_norm, norm_mem, w_mem_kv,
              w_br_conv, w_br_gla, w_br_mem, w_o, norm_ffn, w_router_group, w_router_expert,
              w_gate, w_up, w_down, norm_final):
    for l in range(DEPTH):
        h = rms_norm(x, norm_mix[l])
        x = x + hybrid_mixer(h, mem, w_in[l], conv_w[l], gla_w_a2[l], gla_b_a2[l], gla_norm[l],
                             norm_mem[l], w_mem_kv[l], w_br_conv[l], w_br_gla[l], w_br_mem[l], w_o[l])
        h = rms_norm(x, norm_ffn[l])
        x = x + hierarchical_moe(h, w_router_group[l], w_router_expert[l], w_gate[l], w_up[l], w_down[l])
    return rms_norm(x, norm_final)
```

```python
import functools

import jax
import jax.numpy as jnp
from jax import lax
from jax.experimental import pallas as pl
from jax.experimental.pallas import tpu as pltpu

F32 = jnp.float32
BF16 = jnp.bfloat16

D_MODEL = 2048
BATCH = 4
SEQ = 4096
N_TOK = BATCH * SEQ
N_MEM = 256
EPS = 1e-6
CONV_WIDTH = 1024
GLA_HEADS = 4
GLA_DK = 1024
GLA_DV = 2048
GLA_RANK = 16
GLA_TAU = 16.0
GLA_CHUNK = 64
MEM_HEADS = 4
MEM_WIDTH = 1024
N_GROUPS = 4
EXPERTS_PER_GROUP = 4
N_EXPERTS = 16
D_EXPERT = 512
N_PAIRS = 6
N_CLASSES = N_GROUPS * N_PAIRS
PAIR_LO = (0, 0, 0, 1, 1, 2)
PAIR_HI = (1, 2, 3, 2, 3, 3)

Z_COLS = 3 * CONV_WIDTH + 2 * GLA_DK + 2 * GLA_DV + MEM_WIDTH + 3 * D_MODEL
COL_CB, COL_CC, COL_CV = 0, 1024, 2048
COL_GQ, COL_GK, COL_GV, COL_GR = 3072, 4096, 5120, 7168
COL_MQ = 9216
COL_GATE_A, COL_GATE_B, COL_GATE_M = 10240, 12288, 14336
LANES = 128

INPROJ_TM, INPROJ_TN = 1024, 1024
GLA_ROWS = 256
MERGE_TM = 256
ROUTER_TR = 1024
MOE_TILE = 256
GATHER_ROWS = 256
MOE_MAX_TILES = N_TOK // MOE_TILE + N_CLASSES
N_PAD = MOE_MAX_TILES * MOE_TILE
VMEM_LIMIT = 56 * 1024 * 1024


def _rms(x, g):
    return x * lax.rsqrt(jnp.mean(x * x, axis=-1, keepdims=True) + EPS) * g


def _sigmoid(x):
    return 1.0 / (1.0 + jnp.exp(-x))


def _inproj_kernel(x_ref, g_ref, w_ref, wdec_ref, z_ref, dec_ref, h_scr):
    @pl.when(pl.program_id(1) == 0)
    def _():
        hb = _rms(x_ref[...], g_ref[...]).astype(BF16)
        h_scr[...] = hb
        dec_ref[...] = jnp.dot(hb, wdec_ref[...], preferred_element_type=F32)

    z_ref[...] = jnp.dot(h_scr[...], w_ref[...], preferred_element_type=F32).astype(BF16)


def _inproj(x2d, g, w_main, w_dec):
    tm, tn = INPROJ_TM, INPROJ_TN
    return pl.pallas_call(
        _inproj_kernel,
        grid=(N_TOK // tm, Z_COLS // tn),
        in_specs=[
            pl.BlockSpec((tm, D_MODEL), lambda i, j: (i, 0)),
            pl.BlockSpec((1, D_MODEL), lambda i, j: (0, 0)),
            pl.BlockSpec((D_MODEL, tn), lambda i, j: (0, j)),
            pl.BlockSpec((D_MODEL, LANES), lambda i, j: (0, 0)),
        ],
        out_specs=[
            pl.BlockSpec((tm, tn), lambda i, j: (i, j)),
            pl.BlockSpec((tm, LANES), lambda i, j: (i, 0)),
        ],
        out_shape=[
            jax.ShapeDtypeStruct((N_TOK, Z_COLS), BF16),
            jax.ShapeDtypeStruct((N_TOK, LANES), F32),
        ],
        scratch_shapes=[pltpu.VMEM((tm, D_MODEL), BF16)],
        compiler_params=pltpu.CompilerParams(
            dimension_semantics=("arbitrary", "arbitrary"), vmem_limit_bytes=VMEM_LIMIT),
        name="inproj",
    )(x2d, g, w_main, w_dec)


def _cumsum_rows(x):
    n = x.shape[0]
    row = lax.broadcasted_iota(jnp.int32, x.shape, 0)
    s = 1
    while s < n:
        x = x + jnp.where(row >= s, pltpu.roll(x, s, 0), 0.0)
        s *= 2
    return x


def _gla_kernel(q_ref, k_ref, v_ref, r_ref, dec_ref, wa_ref, ba_ref, gn_ref, o_ref, st_scr):
    @pl.when(pl.program_id(2) == 0)
    def _():
        st_scr[...] = jnp.zeros_like(st_scr)

    dk = q_ref.shape[1]
    pre = jnp.dot(dec_ref[...], wa_ref[...], precision=lax.Precision.HIGHEST,
                  preferred_element_type=F32) + ba_ref[...]
    log_a = (jnp.minimum(pre, 0.0) - jnp.log1p(jnp.exp(-jnp.abs(pre)))) / GLA_TAU
    C = GLA_CHUNK
    ri = lax.broadcasted_iota(jnp.int32, (C, C), 0)
    ci = lax.broadcasted_iota(jnp.int32, (C, C), 1)
    causal = ri >= ci
    for c in range(q_ref.shape[0] // C):
        sl = slice(c * C, (c + 1) * C)
        bcum = _cumsum_rows(log_a[sl])
        b_last = bcum[C - 1:C, :]
        q = q_ref[sl, :].astype(F32) * (dk ** -0.5)
        k = k_ref[sl, :].astype(F32)
        v = v_ref[sl, :]
        q_dec = (q * jnp.exp(bcum)).astype(BF16)
        k_dec = (k * jnp.exp(-bcum)).astype(BF16)
        k_end = (k * jnp.exp(b_last - bcum)).astype(BF16)
        att = lax.dot_general(q_dec, k_dec, (((1,), (1,)), ((), ())), preferred_element_type=F32)
        att = jnp.where(causal, att, 0.0).astype(BF16)
        st = st_scr[...]
        o = jnp.dot(att, v, preferred_element_type=F32)
        o = o + lax.dot_general(q_dec, st.astype(BF16), (((1,), (1,)), ((), ())),
                                preferred_element_type=F32)
        st_scr[...] = st * jnp.exp(b_last) + lax.dot_general(
            v, k_end, (((0,), (0,)), ((), ())), preferred_element_type=F32)
        o = o * lax.rsqrt(jnp.mean(o * o, axis=-1, keepdims=True) + EPS) * gn_ref[...]
        r = r_ref[sl, :].astype(F32)
        o_ref[sl, :] = (o * (r * _sigmoid(r))).astype(BF16)


def _gla(z, dec, wa_pad, ba, gnorm):
    R = GLA_ROWS
    dk, dv = GLA_DK // GLA_HEADS, GLA_DV // GLA_HEADS
    steps = SEQ // R

    def rows(b, h, s):
        return b * steps + s

    return pl.pallas_call(
        _gla_kernel,
        grid=(BATCH, GLA_HEADS, steps),
        in_specs=[
            pl.BlockSpec((R, dk), lambda b, h, s: (rows(b, h, s), COL_GQ // dk + h)),
            pl.BlockSpec((R, dk), lambda b, h, s: (rows(b, h, s), COL_GK // dk + h)),
            pl.BlockSpec((R, dv), lambda b, h, s: (rows(b, h, s), COL_GV // dv + h)),
            pl.BlockSpec((R, dv), lambda b, h, s: (rows(b, h, s), COL_GR // dv + h)),
            pl.BlockSpec((R, LANES), lambda b, h, s: (rows(b, h, s), 0)),
            pl.BlockSpec((LANES, dk), lambda b, h, s: (0, h)),
            pl.BlockSpec((1, dk), lambda b, h, s: (0, h)),
            pl.BlockSpec((1, dv), lambda b, h, s: (0, h)),
        ],
        out_specs=pl.BlockSpec((R, dv), lambda b, h, s: (rows(b, h, s), h)),
        out_shape=jax.ShapeDtypeStruct((N_TOK, GLA_DV), BF16),
        scratch_shapes=[pltpu.VMEM((dv, dk), F32)],
        compiler_params=pltpu.CompilerParams(
            dimension_semantics=("arbitrary", "arbitrary", "arbitrary"),
            vmem_limit_bytes=VMEM_LIMIT),
        name="gla",
    )(z, z, z, z, dec, wa_pad, ba, gnorm)


def _memkv_kernel(m_ref, g_ref, w_ref, o_ref):
    hb = _rms(m_ref[...], g_ref[...]).astype(BF16)
    o_ref[...] = jnp.dot(hb, w_ref[...], preferred_element_type=F32).astype(BF16)


def _memkv(mem2d, g, w):
    return pl.pallas_call(
        _memkv_kernel,
        grid=(BATCH,),
        in_specs=[
            pl.BlockSpec((N_MEM, D_MODEL), lambda b: (b, 0)),
            pl.BlockSpec((1, D_MODEL), lambda b: (0, 0)),
            pl.BlockSpec((D_MODEL, 2 * MEM_WIDTH), lambda b: (0, 0)),
        ],
        out_specs=pl.BlockSpec((N_MEM, 2 * MEM_WIDTH), lambda b: (b, 0)),
        out_shape=jax.ShapeDtypeStruct((BATCH * N_MEM, 2 * MEM_WIDTH), BF16),
        compiler_params=pltpu.CompilerParams(
            dimension_semantics=("arbitrary",), vmem_limit_bytes=VMEM_LIMIT),
        name="memkv",
    )(mem2d, g, w)


HALO = 16


def _merge_kernel(x_ref, cb_ref, cc_ref, cv_ref, cch_ref, cvh_ref, mq_ref, ga_ref, gb_ref,
                  gm_ref, yb_ref, kv_ref, cw_ref, wbc_ref, wbg_ref, wbm_ref, wo_ref, o_ref):
    tm = x_ref.shape[0]
    i = pl.program_id(0)
    u = cc_ref[...].astype(F32) * cv_ref[...].astype(F32)
    uh = cch_ref[...].astype(F32) * cvh_ref[...].astype(F32)
    seq_start = (i * tm) % SEQ == 0
    uh = jnp.where(seq_start, 0.0, uh)
    um1 = uh[HALO - 1:HALO, :]
    um2 = uh[HALO - 2:HALO - 1, :]
    row = lax.broadcasted_iota(jnp.int32, u.shape, 0)
    s1 = jnp.where(row == 0, um1, pltpu.roll(u, 1, 0))
    s2 = jnp.where(row == 0, um2, jnp.where(row == 1, um1, pltpu.roll(u, 2, 0)))
    cw = cw_ref[...]
    y = cw[0:1, :] * u + cw[1:2, :] * s1 + cw[2:3, :] * s2
    ya = (cb_ref[...].astype(F32) * y).astype(BF16)

    hd = MEM_WIDTH // MEM_HEADS
    heads = []
    for h in range(MEM_HEADS):
        qh = mq_ref[:, h * hd:(h + 1) * hd]
        kh = kv_ref[:, h * hd:(h + 1) * hd]
        vh = kv_ref[:, MEM_WIDTH + h * hd:MEM_WIDTH + (h + 1) * hd]
        sc = lax.dot_general(qh, kh, (((1,), (1,)), ((), ())),
                             preferred_element_type=F32) * (hd ** -0.5)
        e = jnp.exp(sc - jnp.max(sc, axis=-1, keepdims=True))
        p = e / jnp.sum(e, axis=-1, keepdims=True)
        heads.append(jnp.dot(p.astype(BF16), vh, preferred_element_type=F32).astype(BF16))
    ym = jnp.concatenate(heads, axis=-1)

    merged = _sigmoid(ga_ref[...].astype(F32)) * jnp.dot(ya, wbc_ref[...], preferred_element_type=F32)
    merged = merged + _sigmoid(gb_ref[...].astype(F32)) * jnp.dot(
        yb_ref[...], wbg_ref[...], preferred_element_type=F32)
    merged = merged + _sigmoid(gm_ref[...].astype(F32)) * jnp.dot(
        ym, wbm_ref[...], preferred_element_type=F32)
    o_ref[...] = x_ref[...] + jnp.dot(merged.astype(BF16), wo_ref[...], preferred_element_type=F32)


def _merge(x2d, z, yb, kv, conv_w, wbc, wbg, wbm, wo):
    tm = MERGE_TM
    cw, dm = CONV_WIDTH, D_MODEL
    halo_blocks = tm // HALO

    def const(shape):
        return pl.BlockSpec(shape, lambda i: (0, 0), pipeline_mode=pl.Buffered(1))

    return pl.pallas_call(
        _merge_kernel,
        grid=(N_TOK // tm,),
        in_specs=[
            pl.BlockSpec((tm, dm), lambda i: (i, 0)),
            pl.BlockSpec((tm, cw), lambda i: (i, COL_CB // cw)),
            pl.BlockSpec((tm, cw), lambda i: (i, COL_CC // cw)),
            pl.BlockSpec((tm, cw), lambda i: (i, COL_CV // cw)),
            pl.BlockSpec((HALO, cw), lambda i: (jnp.maximum(i * halo_blocks - 1, 0), COL_CC // cw)),
            pl.BlockSpec((HALO, cw), lambda i: (jnp.maximum(i * halo_blocks - 1, 0), COL_CV // cw)),
            pl.BlockSpec((tm, MEM_WIDTH), lambda i: (i, COL_MQ // MEM_WIDTH)),
            pl.BlockSpec((tm, dm), lambda i: (i, COL_GATE_A // dm)),
            pl.BlockSpec((tm, dm), lambda i: (i, COL_GATE_B // dm)),
            pl.BlockSpec((tm, dm), lambda i: (i, COL_GATE_M // dm)),
            pl.BlockSpec((tm, GLA_DV), lambda i: (i, 0)),
            pl.BlockSpec((N_MEM, 2 * MEM_WIDTH), lambda i: ((i * tm) // SEQ, 0)),
            const((3, cw)),
            const((cw, dm)),
            const((GLA_DV, dm)),
            const((MEM_WIDTH, dm)),
            const((dm, dm)),
        ],
        out_specs=pl.BlockSpec((tm, dm), lambda i: (i, 0)),
        out_shape=jax.ShapeDtypeStruct((N_TOK, dm), F32),
        compiler_params=pltpu.CompilerParams(
            dimension_semantics=("arbitrary",), vmem_limit_bytes=VMEM_LIMIT),
        name="merge",
    )(x2d, z, z, z, z, z, z, z, z, z, yb, kv, conv_w, wbc, wbg, wbm, wo)


ROUTE_ROWS = 8
ROUTER_W_ROWS = 32
CLS_ROWS = 32


def _softmax_rows(ls):
    m = functools.reduce(jnp.maximum, ls)
    es = [jnp.exp(l - m) for l in ls]
    z = functools.reduce(lambda a, b: a + b, es)
    return [e / z for e in es]


def _argmax_rows(ps):
    best = ps[0]
    idx = jnp.zeros(best.shape, jnp.int32)
    for k in range(1, len(ps)):
        better = ps[k] > best
        idx = jnp.where(better, k, idx)
        best = jnp.where(better, ps[k], best)
    return best, idx


def _router_kernel(x_ref, g_ref, wr_ref, route_ref, cnt_ref, run_scr):
    tr = x_ref.shape[0]

    @pl.when(pl.program_id(0) == 0)
    def _():
        run_scr[...] = jnp.zeros_like(run_scr)

    h = _rms(x_ref[...], g_ref[...])
    logits = lax.dot_general(wr_ref[...], h, (((1,), (1,)), ((), ())),
                             precision=lax.Precision.HIGHEST, preferred_element_type=F32)
    pg = _softmax_rows([logits[k:k + 1, :] for k in range(N_GROUPS)])
    wg, gsel = _argmax_rows(pg)
    le = [logits[8 + e:9 + e, :] for e in range(N_EXPERTS)]
    lsel = []
    for j in range(EXPERTS_PER_GROUP):
        v = le[(N_GROUPS - 1) * EXPERTS_PER_GROUP + j]
        for g in range(N_GROUPS - 2, -1, -1):
            v = jnp.where(gsel == g, le[g * EXPERTS_PER_GROUP + j], v)
        lsel.append(v)
    pe = _softmax_rows(lsel)
    v1, i1 = _argmax_rows(pe)
    v2, i2 = _argmax_rows([jnp.where(i1 == j, -1.0, pe[j]) for j in range(EXPERTS_PER_GROUP)])
    den = v1 + v2
    w1 = wg * (v1 / den)
    w2 = wg * (v2 / den)
    first_lo = i1 < i2
    lo = jnp.where(first_lo, i1, i2)
    hi = jnp.where(first_lo, i2, i1)
    w_lo = jnp.where(first_lo, w1, w2)
    w_hi = jnp.where(first_lo, w2, w1)
    base = jnp.where(lo == 0, 0, jnp.where(lo == 1, 3, 5))
    cls = gsel * N_PAIRS + base + hi - lo - 1

    onehot = lax.broadcasted_iota(jnp.int32, (CLS_ROWS, tr), 0) == cls
    si = lax.broadcasted_iota(jnp.int32, (tr, tr), 0)
    ti = lax.broadcasted_iota(jnp.int32, (tr, tr), 1)
    upper = jnp.where(si <= ti, 1.0, 0.0).astype(BF16)
    cnt = jnp.dot(jnp.where(onehot, 1.0, 0.0).astype(BF16), upper, preferred_element_type=F32)
    run = run_scr[...]
    rank = jnp.sum(jnp.where(onehot, cnt - 1.0 + run, 0.0), axis=0, keepdims=True)
    run = run + cnt[:, tr - 1:tr]
    run_scr[...] = run
    cnt_ref[...] = jnp.broadcast_to(run, cnt_ref.shape)
    zero = jnp.zeros_like(rank)
    route_ref[...] = jnp.concatenate(
        [cls.astype(F32), rank, w_lo, w_hi, zero, zero, zero, zero], axis=0)


def _router(x2d, g, wr):
    tr = ROUTER_TR
    return pl.pallas_call(
        _router_kernel,
        grid=(N_TOK // tr,),
        in_specs=[
            pl.BlockSpec((tr, D_MODEL), lambda i: (i, 0)),
            pl.BlockSpec((1, D_MODEL), lambda i: (0, 0)),
            pl.BlockSpec((ROUTER_W_ROWS, D_MODEL), lambda i: (0, 0)),
        ],
        out_specs=[
            pl.BlockSpec((ROUTE_ROWS, tr), lambda i: (0, i)),
            pl.BlockSpec((CLS_ROWS, LANES), lambda i: (0, 0)),
        ],
        out_shape=[
            jax.ShapeDtypeStruct((ROUTE_ROWS, N_TOK), F32),
            jax.ShapeDtypeStruct((CLS_ROWS, LANES), F32),
        ],
        scratch_shapes=[pltpu.VMEM((CLS_ROWS, 1), F32)],
        compiler_params=pltpu.CompilerParams(
            dimension_semantics=("arbitrary",), vmem_limit_bytes=VMEM_LIMIT),
        name="router",
    )(x2d, g, wr)


def _gather_kernel(idx_ref, nrows_ref, src_ref, g_ref, o_ref, sem, *, norm):
    R = o_ref.shape[0]
    i = pl.program_id(0)

    def row_copy(src_row, r):
        return pltpu.make_async_copy(src_ref.at[pl.ds(src_row, 1), :], o_ref.at[pl.ds(r, 1), :], sem)

    @pl.when(i * R < nrows_ref[0])
    def _():
        base = i * R

        def issue(r, carry):
            row_copy(idx_ref[base + r], r).start()
            return carry

        lax.fori_loop(0, R, issue, 0, unroll=8)

        def wait(r, carry):
            row_copy(0, r).wait()
            return carry

        lax.fori_loop(0, R, wait, 0, unroll=8)
        if norm:
            o_ref[...] = _rms(o_ref[...], g_ref[...])

    @pl.when(i * R >= nrows_ref[0])
    def _():
        o_ref[...] = jnp.zeros_like(o_ref)


def _gather_rows(src, idx, nrows, g, *, norm):
    R = GATHER_ROWS
    n_out = idx.shape[0]
    grid_spec = pltpu.PrefetchScalarGridSpec(
        num_scalar_prefetch=2,
        grid=(n_out // R,),
        in_specs=[
            pl.BlockSpec(memory_space=pl.ANY),
            pl.BlockSpec((1, D_MODEL), lambda i, idx, n: (0, 0)),
        ],
        out_specs=pl.BlockSpec((R, D_MODEL), lambda i, idx, n: (i, 0)),
        scratch_shapes=[pltpu.SemaphoreType.DMA(())],
    )
    return pl.pallas_call(
        functools.partial(_gather_kernel, norm=norm),
        grid_spec=grid_spec,
        out_shape=jax.ShapeDtypeStruct((n_out, D_MODEL), F32),
        compiler_params=pltpu.CompilerParams(
            dimension_semantics=("arbitrary",), vmem_limit_bytes=VMEM_LIMIT),
        name="gather_norm" if norm else "gather",
    )(idx, nrows, src, g)


def _moe_kernel(elo_ref, ehi_ref, ntile_ref, x_ref, g_ref, wl_ref, wh_ref,
                wg_lo, wu_lo, wd_lo, wg_hi, wu_hi, wd_hi, o_ref):
    @pl.when(pl.program_id(0) < ntile_ref[0])
    def _():
        x = x_ref[...]
        hb = _rms(x, g_ref[...]).astype(BF16)

        def expert(wg, wu, wd, w):
            a = jnp.dot(hb, wg[...], preferred_element_type=F32)
            b = jnp.dot(hb, wu[...], preferred_element_type=F32)
            hid = (a * _sigmoid(a) * b).astype(BF16)
            return w * jnp.dot(hid, wd[...], preferred_element_type=F32)

        out = expert(wg_lo, wu_lo, wd_lo, wl_ref[...]) + expert(wg_hi, wu_hi, wd_hi, wh_ref[...])
        o_ref[...] = x + out

    @pl.when(pl.program_id(0) >= ntile_ref[0])
    def _():
        o_ref[...] = jnp.zeros_like(o_ref)


def _moe(xs, g, w_lo, w_hi, e_lo, e_hi, ntile, wgate, wup, wdown):
    T = MOE_TILE

    def tile(i, elo, ehi, nt):
        return (jnp.minimum(i, nt[0] - 1), 0)

    def wspec(shape, which):
        if which == 0:
            return pl.BlockSpec((None,) + shape, lambda i, elo, ehi, nt: (elo[i], 0, 0))
        return pl.BlockSpec((None,) + shape, lambda i, elo, ehi, nt: (ehi[i], 0, 0))

    grid_spec = pltpu.PrefetchScalarGridSpec(
        num_scalar_prefetch=3,
        grid=(MOE_MAX_TILES,),
        in_specs=[
            pl.BlockSpec((T, D_MODEL), tile),
            pl.BlockSpec((1, D_MODEL), lambda i, elo, ehi, nt: (0, 0)),
            pl.BlockSpec((T, 1), tile),
            pl.BlockSpec((T, 1), tile),
            wspec((D_MODEL, D_EXPERT), 0), wspec((D_MODEL, D_EXPERT), 0), wspec((D_EXPERT, D_MODEL), 0),
            wspec((D_MODEL, D_EXPERT), 1), wspec((D_MODEL, D_EXPERT), 1), wspec((D_EXPERT, D_MODEL), 1),
        ],
        out_specs=pl.BlockSpec((T, D_MODEL), lambda i, elo, ehi, nt: (i, 0)),
    )
    return pl.pallas_call(
        _moe_kernel,
        grid_spec=grid_spec,
        out_shape=jax.ShapeDtypeStruct((N_PAD, D_MODEL), F32),
        compiler_params=pltpu.CompilerParams(
            dimension_semantics=("arbitrary",), vmem_limit_bytes=VMEM_LIMIT),
        name="moe",
    )(e_lo, e_hi, ntile, xs, g, w_lo, w_hi, wgate, wup, wdown, wgate, wup, wdown)


def _dispatch_plan(route, counts):
    T = MOE_TILE
    cls = route[0].astype(jnp.int32)
    rank = route[1].astype(jnp.int32)
    cnt = counts[:N_CLASSES, 0].astype(jnp.int32)
    ntile_c = (cnt + T - 1) // T
    tile_end = jnp.cumsum(ntile_c)
    tile_start = tile_end - ntile_c
    pos = (tile_start * T)[cls] + rank
    ntile = tile_end[-1]
    src = jnp.zeros((N_PAD,), jnp.int32).at[pos].set(jnp.arange(N_TOK, dtype=jnp.int32))
    t = jnp.arange(MOE_MAX_TILES, dtype=jnp.int32)
    tcls = jnp.sum((jnp.minimum(t, ntile - 1)[:, None] >= tile_end[None, :]).astype(jnp.int32), axis=1)
    grp, pair = tcls // N_PAIRS, tcls % N_PAIRS
    e_lo = grp * EXPERTS_PER_GROUP + jnp.asarray(PAIR_LO, jnp.int32)[pair]
    e_hi = grp * EXPERTS_PER_GROUP + jnp.asarray(PAIR_HI, jnp.int32)[pair]
    w_lo = route[2][src][:, None]
    w_hi = route[3][src][:, None]
    return pos, src, ntile.reshape(1), e_lo, e_hi, w_lo, w_hi


def _split_w_in(w):
    dec0 = 3 * CONV_WIDTH + 2 * GLA_DK + 2 * GLA_DV
    main = jnp.concatenate([w[:, :dec0], w[:, dec0 + GLA_RANK:]], axis=1).astype(BF16)
    dec = jnp.pad(w[:, dec0:dec0 + GLA_RANK], ((0, 0), (0, LANES - GLA_RANK))).astype(BF16)
    return main, dec


def kernel(x, mem, norm_mix, w_in, conv_w, gla_w_a2, gla_b_a2, gla_norm, norm_mem, w_mem_kv,
           w_br_conv, w_br_gla, w_br_mem, w_o, norm_ffn, w_router_group, w_router_expert,
           w_gate, w_up, w_down, norm_final):
    depth = w_in.shape[0]
    xf = x.reshape(N_TOK, D_MODEL)
    mem2d = mem.reshape(BATCH * N_MEM, D_MODEL)
    all_rows = jnp.full((1,), N_TOK, jnp.int32)
    for l in range(depth):
        w_main, w_dec = _split_w_in(w_in[l])
        z, dec = _inproj(xf, norm_mix[l][None, :], w_main, w_dec)
        wa_pad = jnp.pad(gla_w_a2[l], ((0, LANES - GLA_RANK), (0, 0)))
        yb = _gla(z, dec, wa_pad, gla_b_a2[l][None, :], gla_norm[l][None, :])
        kv = _memkv(mem2d, norm_mem[l][None, :], w_mem_kv[l].astype(BF16))
        x1 = _merge(xf, z, yb, kv, conv_w[l], w_br_conv[l].astype(BF16), w_br_gla[l].astype(BF16),
                    w_br_mem[l].astype(BF16), w_o[l].astype(BF16))

        wr = jnp.zeros((ROUTER_W_ROWS, D_MODEL), F32)
        wr = wr.at[0:N_GROUPS].set(w_router_group[l].T).at[8:8 + N_EXPERTS].set(w_router_expert[l].T)
        gf = norm_ffn[l][None, :]
        route, counts = _router(x1, gf, wr)
        pos, src, ntile, e_lo, e_hi, w_lo, w_hi = _dispatch_plan(route, counts)
        xs = _gather_rows(x1, src, ntile * MOE_TILE, gf, norm=False)
        ys = _moe(xs, gf, w_lo, w_hi, e_lo, e_hi, ntile,
                  w_gate[l].astype(BF16), w_up[l].astype(BF16), w_down[l].astype(BF16))
        last = l == depth - 1
        xf = _gather_rows(ys, pos, all_rows, norm_final[None, :] if last else gf, norm=last)
    return xf.reshape(x.shape)
```
